```python
import jax
import jax.numpy as jnp
from jax import lax
import numpy as np

D_MODEL = 1024
BATCH = 32
SEQ = 2048
DEPTH = 4
DEC_BATCH = 16
DEC_SEQ = 64
PAST_LEN = 4096

CHUNK = 64
HEAD_DIM = 64
A_HEADS = 4
A_WIDTH = A_HEADS * HEAD_DIM
A_BAND_CHUNKS = 8
A_WINDOW = A_BAND_CHUNKS * CHUNK
REL_MAX = 128
REL_SIZE = REL_MAX + CHUNK
B_HEADS = 8
B_WIDTH = B_HEADS * HEAD_DIM
Q_BLOCK = 128
C_GROUPS = 4
C_GROUP_W = 64
C_WIDTH = C_GROUPS * C_GROUP_W
C_CHUNK = 128
N_BRANCH = 3
P_TOTAL = 3 * A_WIDTH + 3 * B_WIDTH + B_HEADS + 2 * C_WIDTH + N_BRANCH * D_MODEL
D_FF = 2752
FFN_RES = 0.5
N_MOD = 9
RMS_EPS = 1e-6
NEG_INF = -1e30
FORGET_BIAS_INIT = 3.0

kernel_name = "hybrid_streaming_encoder_step"


def rmsnorm(x, g):
    xf = x.astype(jnp.float32)
    y = xf * lax.rsqrt(jnp.mean(xf * xf, axis=-1, keepdims=True) + RMS_EPS)
    return (y * g.astype(jnp.float32)).astype(x.dtype)


def adaln(c, w_ada, b_ada):
    return jnp.split(jax.nn.silu(c) @ w_ada + b_ada, N_MOD, axis=-1)


def pre_mod(h, g_pre, shift, scale):
    return rmsnorm(h, g_pre) * (1 + scale[:, None, :]) + shift[:, None, :]


def post_add(h, y, g_post, gate, weight):
    return h + weight * gate[:, None, :] * rmsnorm(y, g_post)


def swiglu(x, w1, w2):
    g, u = jnp.split(x @ w1, 2, axis=-1)
    return (jax.nn.silu(g) * u) @ w2


def in_proj_split(x, w_in, b_forget, c_vnorm_g):
    B, T, _ = x.shape
    sizes = [A_WIDTH] * 3 + [B_WIDTH] * 3 + [B_HEADS] + [C_WIDTH] * 2
    cuts = [int(i) for i in np.cumsum(sizes)]
    qa, ka, va, qb, kb, vb, fb, uc, vc, gl = jnp.split(x @ w_in, cuts, axis=-1)
    def heads(t, h):
        return t.reshape(B, T, h, HEAD_DIM)
    logf = jax.nn.log_sigmoid(fb.astype(jnp.float32) + b_forget.astype(jnp.float32))
    vc = rmsnorm(vc, c_vnorm_g)
    return (heads(qa, A_HEADS), heads(ka, A_HEADS), heads(va, A_HEADS),
            heads(qb, B_HEADS), heads(kb, B_HEADS), heads(vb, B_HEADS), logf, uc, vc, gl)


def band_attention(q, k, v, q_pos, k_pos, rel_bias):
    s = jnp.einsum('bqhd,bkhd->bhqk', q, k).astype(jnp.float32) * (HEAD_DIM ** -0.5)
    rel = jnp.clip(q_pos[:, None] - k_pos[None, :], -(CHUNK - 1), REL_MAX) + (CHUNK - 1)
    s = s + rel_bias.astype(jnp.float32)[:, rel][None]
    qc = q_pos[:, None] // CHUNK
    kc = k_pos[None, :] // CHUNK
    mask = (kc <= qc) & (kc >= qc - A_BAND_CHUNKS) & (k_pos[None, :] >= 0)
    s = jnp.where(mask[None, None], s, NEG_INF)
    p = jax.nn.softmax(s, axis=-1).astype(v.dtype)
    return jnp.einsum('bhqk,bkhd->bqhd', p, v)


def band_attention_prompt(q, k, v, rel_bias):
    B, T = q.shape[:2]
    pad = ((0, 0), (A_WINDOW, 0), (0, 0), (0, 0))
    kp = jnp.pad(k, pad)
    vp = jnp.pad(v, pad)
    band = A_WINDOW + CHUNK
    def one_chunk(c):
        start = c * CHUNK
        qc = lax.dynamic_slice_in_dim(q, start, CHUNK, axis=1)
        kc = lax.dynamic_slice_in_dim(kp, start, band, axis=1)
        vc = lax.dynamic_slice_in_dim(vp, start, band, axis=1)
        q_pos = start + jnp.arange(CHUNK)
        k_pos = start - A_WINDOW + jnp.arange(band)
        return band_attention(qc, kc, vc, q_pos, k_pos, rel_bias)
    out = lax.map(one_chunk, jnp.arange(T // CHUNK))
    return jnp.moveaxis(out, 0, 1).reshape(B, T, A_HEADS, HEAD_DIM)


def fox_block(q, k, v, f_q, f_k, q_pos, k_pos):
    s = jnp.einsum('bqhd,bkhd->bhqk', q, k).astype(jnp.float32) * (HEAD_DIM ** -0.5)
    s = s + jnp.swapaxes(f_q, 1, 2)[..., :, None] - jnp.swapaxes(f_k, 1, 2)[..., None, :]
    mask = k_pos[None, :] <= q_pos[:, None]
    s = jnp.where(mask[None, None], s, NEG_INF)
    p = jax.nn.softmax(s, axis=-1).astype(v.dtype)
    return jnp.einsum('bhqk,bkhd->bqhd', p, v)


def fox_prompt(q, k, v, logf):
    B, T = q.shape[:2]
    F = jnp.cumsum(logf, axis=1)
    pos = jnp.arange(T)
    def one_block(i):
        start = i * Q_BLOCK
        qb = lax.dynamic_slice_in_dim(q, start, Q_BLOCK, axis=1)
        fq = lax.dynamic_slice_in_dim(F, start, Q_BLOCK, axis=1)
        return fox_block(qb, k, v, fq, F, start + jnp.arange(Q_BLOCK), pos)
    out = lax.map(one_block, jnp.arange(T // Q_BLOCK))
    return jnp.moveaxis(out, 0, 1).reshape(B, T, B_HEADS, HEAD_DIM)


def spatial_gating(u, v, w_s, b_s):
    B, T, _ = u.shape
    L = min(T, C_CHUNK)
    tri = jnp.tril(jnp.ones((L, L), w_s.dtype))
    wm = w_s[:, :L, :L] * tri
    vr = v.reshape(B, T // L, L, C_GROUPS, C_GROUP_W)
    mixed = jnp.einsum('gij,bnjgc->bnigc', wm, vr) + jnp.swapaxes(b_s[:, :L], 0, 1)[None, None, :, :, None]
    return u * mixed.reshape(B, T, C_WIDTH)


def merge_branches(oa, ob, oc, gl, b_gate, w_br_a, w_br_b, w_br_c, w_out):
    B, T = oa.shape[:2]
    ga, gb, gc = jnp.split(jax.nn.sigmoid(gl + b_gate), N_BRANCH, axis=-1)
    m = (ga * (oa.reshape(B, T, A_WIDTH) @ w_br_a)
         + gb * (ob.reshape(B, T, B_WIDTH) @ w_br_b)
         + gc * (oc @ w_br_c))
    return m @ w_out


def mixer_prompt(x, w_in, b_forget, a_rel_bias, c_vnorm_g, c_spatial_w, c_spatial_b,
                 b_gate, w_br_a, w_br_b, w_br_c, w_out):
    T = x.shape[1]
    qa, ka, va, qb, kb, vb, logf, uc, vc, gl = in_proj_split(x, w_in, b_forget, c_vnorm_g)
    oa = band_attention_prompt(qa, ka, va, a_rel_bias)
    ob = fox_prompt(qb, kb, vb, logf)
    oc = spatial_gating(uc, vc, c_spatial_w, c_spatial_b)
    y = merge_branches(oa, ob, oc, gl, b_gate, w_br_a, w_br_b, w_br_c, w_out)
    rows = min(A_WINDOW, T)
    return y, (ka[:, T - rows:], va[:, T - rows:], kb, vb, logf)


def mixer_sample(x, a_k, a_v, b_k, b_v, b_logf, w_in, b_forget, a_rel_bias, c_vnorm_g,
                 c_spatial_w, c_spatial_b, b_gate, w_br_a, w_br_b, w_br_c, w_out):
    Tn = x.shape[1]
    past = b_k.shape[1]
    rows = a_k.shape[1]
    qa, ka, va, qb, kb, vb, logf, uc, vc, gl = in_proj_split(x, w_in, b_forget, c_vnorm_g)
    q_pos = past + jnp.arange(Tn)
    ka_all = jnp.concatenate([a_k, ka], axis=1)
    va_all = jnp.concatenate([a_v, va], axis=1)
    oa = band_attention(qa, ka_all, va_all, q_pos, past - rows + jnp.arange(rows + Tn), a_rel_bias)
    kb_all = jnp.concatenate([b_k, kb], axis=1)
    vb_all = jnp.concatenate([b_v, vb], axis=1)
    F = jnp.cumsum(jnp.concatenate([b_logf.astype(jnp.float32), logf], axis=1), axis=1)
    ob = fox_block(qb, kb_all, vb_all, F[:, past:], F, q_pos, jnp.arange(past + Tn))
    oc = spatial_gating(uc, vc, c_spatial_w, c_spatial_b)
    y = merge_branches(oa, ob, oc, gl, b_gate, w_br_a, w_br_b, w_br_c, w_out)
    return y, (ka_all[:, Tn:], va_all[:, Tn:], kb, vb, logf, vc)


def setup_inputs(seed: int = 0) -> dict:
    key = jax.random.key(seed)
    ks = jax.random.split(key, 26)
    def nrm(k, shape, s=1.0):
        return jax.random.normal(k, shape, jnp.float32) * s
    a_rows = min(A_WINDOW, PAST_LEN)
    return {
        "x_prompt": nrm(ks[0], (BATCH, SEQ, D_MODEL)),
        "x_sample": nrm(ks[1], (DEC_BATCH, DEC_SEQ, D_MODEL)),
        "c_prompt": nrm(ks[2], (BATCH, D_MODEL)),
        "c_sample": nrm(ks[3], (DEC_BATCH, D_MODEL)),
        "cache_a_k": nrm(ks[4], (DEPTH, DEC_BATCH, a_rows, A_HEADS, HEAD_DIM)),
        "cache_a_v": nrm(ks[5], (DEPTH, DEC_BATCH, a_rows, A_HEADS, HEAD_DIM)),
        "cache_b_k": nrm(ks[6], (DEPTH, DEC_BATCH, PAST_LEN, B_HEADS, HEAD_DIM)),
        "cache_b_v": nrm(ks[7], (DEPTH, DEC_BATCH, PAST_LEN, B_HEADS, HEAD_DIM)),
        "cache_b_logf": jax.nn.log_sigmoid(FORGET_BIAS_INIT + nrm(ks[8], (DEPTH, DEC_BATCH, PAST_LEN, B_HEADS))),
        "w_ada": nrm(ks[9], (DEPTH, D_MODEL, N_MOD * D_MODEL), 0.5 * D_MODEL ** -0.5),
        "b_ada": nrm(ks[10], (DEPTH, N_MOD * D_MODEL), 0.02),
        "norm_g": 1.0 + nrm(ks[11], (DEPTH, 6, D_MODEL), 0.01),
        "ffn_w1": nrm(ks[12], (DEPTH, 2, D_MODEL, 2 * D_FF), D_MODEL ** -0.5),
        "ffn_w2": nrm(ks[13], (DEPTH, 2, D_FF, D_MODEL), D_FF ** -0.5),
        "w_in": nrm(ks[14], (DEPTH, D_MODEL, P_TOTAL), D_MODEL ** -0.5),
        "b_forget": FORGET_BIAS_INIT + nrm(ks[15], (DEPTH, B_HEADS), 0.1),
        "b_gate": nrm(ks[16], (DEPTH, N_BRANCH * D_MODEL), 0.02),
        "a_rel_bias": nrm(ks[17], (DEPTH, A_HEADS, REL_SIZE), 0.5),
        "c_vnorm_g": 1.0 + nrm(ks[18], (DEPTH, C_WIDTH), 0.01),
        "c_spatial_w": nrm(ks[19], (DEPTH, C_GROUPS, C_CHUNK, C_CHUNK), C_CHUNK ** -0.5),
        "c_spatial_b": 1.0 + nrm(ks[20], (DEPTH, C_GROUPS, C_CHUNK), 0.1),
        "w_br_a": nrm(ks[21], (DEPTH, A_WIDTH, D_MODEL), A_WIDTH ** -0.5),
        "w_br_b": nrm(ks[22], (DEPTH, B_WIDTH, D_MODEL), B_WIDTH ** -0.5),
        "w_br_c": nrm(ks[23], (DEPTH, C_WIDTH, D_MODEL), C_WIDTH ** -0.5),
        "w_out": nrm(ks[24], (DEPTH, D_MODEL, D_MODEL), D_MODEL ** -0.5),
    }


def reference(x_prompt, x_sample, c_prompt, c_sample, cache_a_k, cache_a_v, cache_b_k, cache_b_v,
              cache_b_logf, w_ada, b_ada, norm_g, ffn_w1, ffn_w2, w_in, b_forget, b_gate, a_rel_bias,
              c_vnorm_g, c_spatial_w, c_spatial_b, w_br_a, w_br_b, w_br_c, w_out):
    hp, hs = x_prompt, x_sample
    sp = [[] for _ in range(5)]
    ss = [[] for _ in range(6)]
    for l in range(DEPTH):
        mp = adaln(c_prompt, w_ada[l], b_ada[l])
        ms = adaln(c_sample, w_ada[l], b_ada[l])
        g = norm_g[l]
        mw = (w_in[l], b_forget[l], a_rel_bias[l], c_vnorm_g[l], c_spatial_w[l], c_spatial_b[l],
              b_gate[l], w_br_a[l], w_br_b[l], w_br_c[l], w_out[l])
        hp = post_add(hp, swiglu(pre_mod(hp, g[0], mp[0], mp[1]), ffn_w1[l, 0], ffn_w2[l, 0]), g[1], mp[2], FFN_RES)
        hs = post_add(hs, swiglu(pre_mod(hs, g[0], ms[0], ms[1]), ffn_w1[l, 0], ffn_w2[l, 0]), g[1], ms[2], FFN_RES)
        yp, st_p = mixer_prompt(pre_mod(hp, g[2], mp[3], mp[4]), *mw)
        hp = post_add(hp, yp, g[3], mp[5], 1.0)
        ys, st_s = mixer_sample(pre_mod(hs, g[2], ms[3], ms[4]), cache_a_k[l], cache_a_v[l], cache_b_k[l],
                                cache_b_v[l], cache_b_logf[l], *mw)
        hs = post_add(hs, ys, g[3], ms[5], 1.0)
        hp = post_add(hp, swiglu(pre_mod(hp, g[4], mp[6], mp[7]), ffn_w1[l, 1], ffn_w2[l, 1]), g[5], mp[8], FFN_RES)
        hs = post_add(hs, swiglu(pre_mod(hs, g[4], ms[6], ms[7]), ffn_w1[l, 1], ffn_w2[l, 1]), g[5], ms[8], FFN_RES)
        for lst, st in zip(sp, st_p):
            lst.append(st)
        for lst, st in zip(ss, st_s):
            lst.append(st)
    new_a_k_prompt, new_a_v_prompt, new_b_k_prompt, new_b_v_prompt, new_b_logf_prompt = [jnp.stack(s, 0) for s in sp]
    new_a_k_sample, new_a_v_sample, new_b_k_sample, new_b_v_sample, new_b_logf_sample, new_c_v_sample = [jnp.stack(s, 0) for s in ss]
    return (hp, hs, new_a_k_prompt, new_a_v_prompt, new_b_k_prompt, new_b_v_prompt, new_b_logf_prompt,
            new_a_k_sample, new_a_v_sample, new_b_k_sample, new_b_v_sample, new_b_logf_sample, new_c_v_sample)
```

```python
import functools

import numpy as np
import jax
import jax.numpy as jnp
from jax import lax
from jax.experimental import pallas as pl
from jax.experimental.pallas import tpu as pltpu

F32 = jnp.float32
BF16 = jnp.bfloat16

CHUNK = 64
HEAD_DIM = 64
A_HEADS = 4
A_WIDTH = A_HEADS * HEAD_DIM
A_BAND_CHUNKS = 8
A_WINDOW = A_BAND_CHUNKS * CHUNK
REL_MAX = 128
B_HEADS = 8
B_WIDTH = B_HEADS * HEAD_DIM
C_GROUPS = 4
C_GROUP_W = 64
C_WIDTH = C_GROUPS * C_GROUP_W
C_CHUNK = 128
N_BRANCH = 3
N_MOD = 9
FFN_RES = 0.5
RMS_EPS = 1e-6
NEG_INF = -1e30
QK_SCALE = HEAD_DIM ** -0.5

LANES = 128
MXU_WIDTH = 256
PAIR = 2 * HEAD_DIM
VMEM_LIMIT = 56 * 1024 * 1024

OFF_A = 0
OFF_B = 3 * A_WIDTH
OFF_UV = OFF_B + 3 * B_WIDTH
OFF_F = OFF_UV + 2 * C_WIDTH
W_SMALL = OFF_F + LANES


def _cparams(sem):
    return pltpu.CompilerParams(dimension_semantics=sem, vmem_limit_bytes=VMEM_LIMIT)


def _const_spec(shape):
    nd = len(shape)
    return pl.BlockSpec(shape, lambda *_: (0,) * nd, pipeline_mode=pl.Buffered(1))


def _rms(x, g):
    return x * lax.rsqrt(jnp.mean(x * x, axis=-1, keepdims=True) + RMS_EPS) * g


def _dot(a, b):
    return jnp.dot(a, b, preferred_element_type=F32)


def _dot_nt(a, b):
    return lax.dot_general(a, b, (((1,), (1,)), ((), ())), preferred_element_type=F32)


def _pre_mod(h_ref, mod_ref, g_ref, k):
    x = h_ref[...]
    bb, tt, d = x.shape
    shift = mod_ref[:, 3 * k:3 * k + 1, :]
    scale = mod_ref[:, 3 * k + 1:3 * k + 2, :]
    xm = _rms(x, g_ref[2 * k:2 * k + 1, :]) * (1.0 + scale) + shift
    return x, xm.reshape(bb * tt, d).astype(BF16)


def _post_add(x, y, mod_ref, g_ref, k, weight):
    bb, tt, d = x.shape
    gate = mod_ref[:, 3 * k + 2:3 * k + 3, :]
    yn = _rms(y, g_ref[2 * k + 1:2 * k + 2, :]).reshape(bb, tt, d)
    return x + (weight * gate) * yn


def _ada_kernel(c_ref, w_ref, b_ref, o_ref):
    c = c_ref[...]
    a = (c * jax.nn.sigmoid(c)).astype(BF16)
    o_ref[0] = _dot(a, w_ref[0]) + b_ref[0]


def _ada(c_all, w_ada, b_ada):
    depth, d, n = w_ada.shape
    nb = c_all.shape[0]
    tn = n // 4
    return pl.pallas_call(
        _ada_kernel,
        grid=(depth, n // tn),
        in_specs=[
            pl.BlockSpec((nb, d), lambda l, j: (0, 0)),
            pl.BlockSpec((1, d, tn), lambda l, j: (l, 0, j)),
            pl.BlockSpec((1, 1, tn), lambda l, j: (l, 0, j)),
        ],
        out_specs=pl.BlockSpec((1, nb, tn), lambda l, j: (l, 0, j)),
        out_shape=jax.ShapeDtypeStruct((depth, nb, n), F32),
        compiler_params=_cparams(("arbitrary", "arbitrary")),
        name="ada",
    )(c_all, w_ada, b_ada.reshape(depth, 1, n))


def _ffn_kernel(k, h_ref, mod_ref, g_ref, w1g_ref, w1u_ref, w2_ref, o_ref, xm_scr, acc_scr):
    x, xm = _pre_mod(h_ref, mod_ref, g_ref, k)
    xm_scr[...] = xm
    acc_scr[...] = jnp.zeros_like(acc_scr)

    def body(c, carry):
        xb = xm_scr[...]
        g = _dot(xb, w1g_ref[c])
        u = _dot(xb, w1u_ref[c])
        a = (g * jax.nn.sigmoid(g) * u).astype(BF16)
        acc_scr[...] += _dot(a, w2_ref[c])
        return carry

    lax.fori_loop(0, w2_ref.shape[0], body, 0)
    o_ref[...] = _post_add(x, acc_scr[...], mod_ref, g_ref, k, FFN_RES)


def _ffn(h, mods, g, w1g, w1u, w2, k, bb, tt):
    b, t, d = h.shape
    m = bb * tt
    return pl.pallas_call(
        functools.partial(_ffn_kernel, k),
        grid=(b // bb, t // tt),
        in_specs=[
            pl.BlockSpec((bb, tt, d), lambda i, j: (i, j, 0)),
            pl.BlockSpec((bb, N_MOD, d), lambda i, j: (i, 0, 0)),
            _const_spec(g.shape),
            _const_spec(w1g.shape),
            _const_spec(w1u.shape),
            _const_spec(w2.shape),
        ],
        out_specs=pl.BlockSpec((bb, tt, d), lambda i, j: (i, j, 0)),
        out_shape=jax.ShapeDtypeStruct(h.shape, F32),
        scratch_shapes=[pltpu.VMEM((m, d), BF16), pltpu.VMEM((m, d), F32)],
        compiler_params=_cparams(("arbitrary", "arbitrary")),
        name="ffn",
    )(h, mods, g, w1g, w1u, w2)


def _inproj_kernel(chunk_len, emit_vc, h_ref, mod_ref, g_ref, w_ref, bf_ref, cg_ref, ws_ref, sb_ref,
                   qa_ref, ka_ref, va_ref, qb_ref, kb_ref, vb_ref, lf_ref, oc_ref, *vc_ref):
    _, xm = _pre_mod(h_ref, mod_ref, g_ref, 1)
    bb, tt, _ = h_ref.shape
    m = bb * tt

    pa = _dot(xm, w_ref[:, OFF_A:OFF_B])
    qa_ref[...] = pa[:, :A_WIDTH].astype(BF16).reshape(bb, tt, A_WIDTH)
    ka_ref[...] = pa[:, A_WIDTH:2 * A_WIDTH].reshape(bb, tt, A_WIDTH)
    va_ref[...] = pa[:, 2 * A_WIDTH:].reshape(bb, tt, A_WIDTH)

    pb = _dot(xm, w_ref[:, OFF_B:OFF_UV])
    qb_ref[...] = pb[:, :B_WIDTH].astype(BF16).reshape(bb, tt, B_WIDTH)
    kb_ref[...] = pb[:, B_WIDTH:2 * B_WIDTH].reshape(bb, tt, B_WIDTH)
    vb_ref[...] = pb[:, 2 * B_WIDTH:].reshape(bb, tt, B_WIDTH)

    fb = _dot(xm, w_ref[:, OFF_F:W_SMALL])[:, :B_HEADS] + bf_ref[...]
    lf_ref[...] = jax.nn.log_sigmoid(fb).reshape(bb, tt, B_HEADS)

    puv = _dot(xm, w_ref[:, OFF_UV:OFF_F])
    u = puv[:, :C_WIDTH]
    vc = _rms(puv[:, C_WIDTH:], cg_ref[...])
    if emit_vc:
        vc_ref[0][...] = vc.reshape(bb, tt, C_WIDTH)
    vcb = vc.astype(BF16)

    row = lax.broadcasted_iota(jnp.int32, (chunk_len, chunk_len), 0)
    col = lax.broadcasted_iota(jnp.int32, (chunk_len, chunk_len), 1)
    wms = [jnp.where(row >= col, ws_ref[gi], 0.0).astype(BF16) for gi in range(C_GROUPS)]
    lane_group = lax.broadcasted_iota(jnp.int32, (chunk_len, C_WIDTH), 1) // C_GROUP_W
    pieces = []
    for r in range(m // chunk_len):
        vr = vcb[r * chunk_len:(r + 1) * chunk_len, :]
        mixed = sb_ref[...]
        for gi in range(C_GROUPS):
            mixed = mixed + jnp.where(lane_group == gi, _dot(wms[gi], vr), 0.0)
        pieces.append(mixed)
    mixed = jnp.concatenate(pieces, axis=0)
    oc_ref[...] = (u * mixed).astype(BF16).reshape(bb, tt, C_WIDTH)


def _inproj(h, mods, g, w_small, b_forget, c_vnorm_g, ws, sb, chunk_len, emit_vc, bb, tt):
    b, t, d = h.shape

    def tok(n, dt):
        return (pl.BlockSpec((bb, tt, n), lambda i, j: (i, j, 0)), jax.ShapeDtypeStruct((b, t, n), dt))

    outs = [tok(A_WIDTH, BF16), tok(A_WIDTH, F32), tok(A_WIDTH, F32),
            tok(B_WIDTH, BF16), tok(B_WIDTH, F32), tok(B_WIDTH, F32),
            tok(B_HEADS, F32), tok(C_WIDTH, BF16)]
    if emit_vc:
        outs.append(tok(C_WIDTH, F32))
    return pl.pallas_call(
        functools.partial(_inproj_kernel, chunk_len, emit_vc),
        grid=(b // bb, t // tt),
        in_specs=[
            pl.BlockSpec((bb, tt, d), lambda i, j: (i, j, 0)),
            pl.BlockSpec((bb, N_MOD, d), lambda i, j: (i, 0, 0)),
            _const_spec(g.shape),
            _const_spec(w_small.shape),
            _const_spec(b_forget.shape),
            _const_spec(c_vnorm_g.shape),
            _const_spec(ws.shape),
            _const_spec(sb.shape),
        ],
        out_specs=[o[0] for o in outs],
        out_shape=[o[1] for o in outs],
        compiler_params=_cparams(("arbitrary", "arbitrary")),
        name="inproj",
    )(h, mods, g, w_small, b_forget, c_vnorm_g, ws, sb)


def _cumsum_kernel(blk, x_ref, o_ref):
    t = x_ref.shape[2]
    row = lax.broadcasted_iota(jnp.int32, (blk, blk), 0)
    col = lax.broadcasted_iota(jnp.int32, (blk, blk), 1)
    tri = (row <= col).astype(F32)
    carry = jnp.zeros((x_ref.shape[1], 1), F32)
    for j in range(t // blk):
        y = jnp.dot(x_ref[0, :, j * blk:(j + 1) * blk], tri, preferred_element_type=F32,
                    precision=lax.Precision.HIGHEST) + carry
        o_ref[0, :, j * blk:(j + 1) * blk] = y
        carry = y[:, blk - 1:blk]


def _cumsum(x):
    b, hh, t = x.shape
    blk = MXU_WIDTH
    return pl.pallas_call(
        functools.partial(_cumsum_kernel, blk),
        grid=(b,),
        in_specs=[pl.BlockSpec((1, hh, t), lambda i: (i, 0, 0))],
        out_specs=pl.BlockSpec((1, hh, t), lambda i: (i, 0, 0)),
        out_shape=jax.ShapeDtypeStruct(x.shape, F32),
        compiler_params=_cparams(("arbitrary",)),
        name="cumsum",
    )(x)


def _head_select(q, hh):
    lane = lax.broadcasted_iota(jnp.int32, q.shape, 1)
    return jnp.where((lane >= HEAD_DIM) == bool(hh), q, jnp.zeros_like(q))


def _merge_heads(o0, o1):
    lane = lax.broadcasted_iota(jnp.int32, o0.shape, 1)
    return jnp.where(lane < HEAD_DIM, o0, o1)


def _online_step(carry, s, v):
    m, l, acc = carry
    m_new = jnp.maximum(m, jnp.max(s, axis=-1, keepdims=True))
    alpha = jnp.exp(m - m_new)
    p = jnp.exp(s - m_new)
    l = alpha * l + jnp.sum(p, axis=-1, keepdims=True)
    acc = alpha * acc + _dot(p.astype(BF16), v)
    return m_new, l, acc


def _softmax_init(tq):
    return (jnp.full((tq, 1), NEG_INF, F32), jnp.zeros((tq, 1), F32), jnp.zeros((tq, PAIR), F32))


def _causal(s):
    row = lax.broadcasted_iota(jnp.int32, s.shape, 0)
    col = lax.broadcasted_iota(jnp.int32, s.shape, 1)
    return jnp.where(col <= row, s, NEG_INF)


def _fox_prompt_kernel(q_ref, k_ref, v_ref, fc_ref, fr_ref, o_ref, kb_scr, vb_scr):
    qi = pl.program_id(2)
    tq = q_ref.shape[1]

    @pl.when(qi == 0)
    def _():
        kb_scr[...] = k_ref[0].astype(BF16)
        vb_scr[...] = v_ref[0].astype(BF16)

    q = q_ref[0] * QK_SCALE
    outs = []
    for hh in range(2):
        qh = _head_select(q, hh)
        fq = fc_ref[0, 0, :, hh:hh + 1]

        def scores(j):
            start = pl.multiple_of(j * tq, tq)
            kj = kb_scr[pl.ds(start, tq), :]
            vj = vb_scr[pl.ds(start, tq), :]
            fk = fr_ref[0, 0, j, hh:hh + 1, :]
            return _dot_nt(qh, kj) + (fq - fk), vj

        def body(j, carry):
            s, vj = scores(j)
            return _online_step(carry, s, vj)

        carry = lax.fori_loop(0, qi, body, _softmax_init(tq))
        s, vj = scores(qi)
        _, l, acc = _online_step(carry, _causal(s), vj)
        outs.append(acc / l)
    o_ref[0] = _merge_heads(outs[0], outs[1]).astype(BF16)


def _fox_prompt(q, k, v, f_col, f_row, tq):
    b, t, w = q.shape
    npair = w // PAIR
    nq = t // tq
    return pl.pallas_call(
        _fox_prompt_kernel,
        grid=(b, npair, nq),
        in_specs=[
            pl.BlockSpec((1, tq, PAIR), lambda i, p, j: (i, j, p)),
            pl.BlockSpec((1, t, PAIR), lambda i, p, j: (i, 0, p)),
            pl.BlockSpec((1, t, PAIR), lambda i, p, j: (i, 0, p)),
            pl.BlockSpec((1, 1, tq, 2), lambda i, p, j: (i, p, j, 0)),
            pl.BlockSpec((1, 1, nq, 2, tq), lambda i, p, j: (i, p, 0, 0, 0)),
        ],
        out_specs=pl.BlockSpec((1, tq, PAIR), lambda i, p, j: (i, j, p)),
        out_shape=jax.ShapeDtypeStruct(q.shape, BF16),
        scratch_shapes=[pltpu.VMEM((t, PAIR), BF16), pltpu.VMEM((t, PAIR), BF16)],
        compiler_params=_cparams(("arbitrary", "arbitrary", "arbitrary")),
        name="fox_prompt",
    )(q, k, v, f_col, f_row)


def _fox_sample_kernel(tk, q_ref, kp_ref, vp_ref, kn_ref, vn_ref, fc_ref, frp_ref, frn_ref, o_ref):
    tq = q_ref.shape[1]
    n_past = kp_ref.shape[2] // tk
    q = q_ref[0] * QK_SCALE
    kn = kn_ref[0].astype(BF16)
    vn = vn_ref[0].astype(BF16)
    outs = []
    for hh in range(2):
        qh = _head_select(q, hh)
        fq = fc_ref[0, 0, :, hh:hh + 1]

        def body(j, carry):
            start = pl.multiple_of(j * tk, tk)
            kj = kp_ref[0, 0, pl.ds(start, tk), :].astype(BF16)
            vj = vp_ref[0, 0, pl.ds(start, tk), :].astype(BF16)
            fk = frp_ref[0, 0, j, hh:hh + 1, :]
            return _online_step(carry, _dot_nt(qh, kj) + (fq - fk), vj)

        carry = lax.fori_loop(0, n_past, body, _softmax_init(tq))
        s = _dot_nt(qh, kn) + (fq - frn_ref[0, 0, hh:hh + 1, :])
        _, l, acc = _online_step(carry, _causal(s), vn)
        outs.append(acc / l)
    o_ref[0] = _merge_heads(outs[0], outs[1]).astype(BF16)


def _fox_sample(q, k_past, v_past, layer, k_new, v_new, f_col, f_row_past, f_row_new, tk):
    b, tq, w = q.shape
    npair = w // PAIR
    past = k_past.shape[2]
    return pl.pallas_call(
        functools.partial(_fox_sample_kernel, tk),
        grid=(b, npair),
        in_specs=[
            pl.BlockSpec((1, tq, PAIR), lambda i, p: (i, 0, p)),
            pl.BlockSpec((1, 1, past, PAIR), lambda i, p: (layer, i, 0, p)),
            pl.BlockSpec((1, 1, past, PAIR), lambda i, p: (layer, i, 0, p)),
            pl.BlockSpec((1, tq, PAIR), lambda i, p: (i, 0, p)),
            pl.BlockSpec((1, tq, PAIR), lambda i, p: (i, 0, p)),
            pl.BlockSpec((1, 1, tq, 2), lambda i, p: (i, p, 0, 0)),
            pl.BlockSpec((1, 1, past // tk, 2, tk), lambda i, p: (i, p, 0, 0, 0)),
            pl.BlockSpec((1, 1, 2, tq), lambda i, p: (i, p, 0, 0)),
        ],
        out_specs=pl.BlockSpec((1, tq, PAIR), lambda i, p: (i, 0, p)),
        out_shape=jax.ShapeDtypeStruct(q.shape, BF16),
        compiler_params=_cparams(("arbitrary", "arbitrary")),
        name="fox_sample",
    )(q, k_past, v_past, k_new, v_new, f_col, f_row_past, f_row_new)


def _band_kernel(pad, window, q_ref, k_ref, v_ref, tbl_ref, o_ref, kp_scr, vp_scr):
    qi = pl.program_id(2)
    rows = q_ref.shape[1]

    @pl.when(qi == 0)
    def _():
        if pad:
            kp_scr[0:pad, :] = jnp.zeros((pad, PAIR), BF16)
            vp_scr[0:pad, :] = jnp.zeros((pad, PAIR), BF16)
        kp_scr[pad:, :] = k_ref[0].astype(BF16)
        vp_scr[pad:, :] = v_ref[0].astype(BF16)

    start = pl.multiple_of(qi * rows, rows)
    kw = kp_scr[pl.ds(start, window), :]
    vw = vp_scr[pl.ds(start, window), :]
    q = q_ref[0] * QK_SCALE
    outs = []
    for hh in range(2):
        s = _dot_nt(_head_select(q, hh), kw) + tbl_ref[hh]
        if pad:
            col = lax.broadcasted_iota(jnp.int32, s.shape, 1)
            s = jnp.where(col + start >= pad, s, NEG_INF)
        p = jnp.exp(s - jnp.max(s, axis=-1, keepdims=True))
        l = jnp.sum(p, axis=-1, keepdims=True)
        outs.append(_dot(p.astype(BF16), vw) / l)
    o_ref[0] = _merge_heads(outs[0], outs[1]).astype(BF16)


def _band(q, k, v, tbl, pad, rows):
    b, t, w = q.shape
    tk = k.shape[1]
    npair = w // PAIR
    window = tbl.shape[2]
    return pl.pallas_call(
        functools.partial(_band_kernel, pad, window),
        grid=(b, npair, t // rows),
        in_specs=[
            pl.BlockSpec((1, rows, PAIR), lambda i, p, j: (i, j, p)),
            pl.BlockSpec((1, tk, PAIR), lambda i, p, j: (i, 0, p)),
            pl.BlockSpec((1, tk, PAIR), lambda i, p, j: (i, 0, p)),
            pl.BlockSpec((2, rows, window), lambda i, p, j: (p, 0, 0)),
        ],
        out_specs=pl.BlockSpec((1, rows, PAIR), lambda i, p, j: (i, j, p)),
        out_shape=jax.ShapeDtypeStruct(q.shape, BF16),
        scratch_shapes=[pltpu.VMEM((pad + tk, PAIR), BF16), pltpu.VMEM((pad + tk, PAIR), BF16)],
        compiler_params=_cparams(("arbitrary", "arbitrary", "arbitrary")),
        name="band",
    )(q, k, v, tbl)


def _band_table(rel_bias, rows, window, back):
    i = np.arange(rows)[:, None]
    kp = np.arange(window)[None, :] - back
    rel = np.clip(i - kp, -(CHUNK - 1), REL_MAX) + (CHUNK - 1)
    qc = i // CHUNK
    kc = np.floor_divide(kp, CHUNK)
    mask = (kc <= qc) & (kc >= qc - A_BAND_CHUNKS)
    return jnp.where(jnp.asarray(mask)[None], rel_bias[:, rel], NEG_INF).astype(F32)


def _merge_kernel(h_ref, mod_ref, g_ref, oa_ref, ob_ref, oc_ref, wgl_ref, bg_ref, wa_ref, wb_ref, wc_ref,
                  wo_ref, o_ref):
    x, xm = _pre_mod(h_ref, mod_ref, g_ref, 1)
    bb, tt, d = x.shape
    m = bb * tt
    mix = None
    for idx, (br_ref, w_ref) in enumerate(((oa_ref, wa_ref), (ob_ref, wb_ref), (oc_ref, wc_ref))):
        gate = jax.nn.sigmoid(_dot(xm, wgl_ref[:, idx * d:(idx + 1) * d]) + bg_ref[:, idx * d:(idx + 1) * d])
        br = br_ref[...]
        term = gate * _dot(br.reshape(m, br.shape[-1]), w_ref[...])
        mix = term if mix is None else mix + term
    y = _dot(mix.astype(BF16), wo_ref[...])
    o_ref[...] = _post_add(x, y, mod_ref, g_ref, 1, 1.0)


def _merge(h, mods, g, oa, ob, oc, wgl, bg, wa, wb, wc, wo, bb, tt):
    b, t, d = h.shape

    def tok(n):
        return pl.BlockSpec((bb, tt, n), lambda i, j: (i, j, 0))

    return pl.pallas_call(
        _merge_kernel,
        grid=(b // bb, t // tt),
        in_specs=[
            tok(d),
            pl.BlockSpec((bb, N_MOD, d), lambda i, j: (i, 0, 0)),
            _const_spec(g.shape),
            tok(A_WIDTH), tok(B_WIDTH), tok(C_WIDTH),
            _const_spec(wgl.shape), _const_spec(bg.shape), _const_spec(wa.shape),
            _const_spec(wb.shape), _const_spec(wc.shape), _const_spec(wo.shape),
        ],
        out_specs=tok(d),
        out_shape=jax.ShapeDtypeStruct(h.shape, F32),
        compiler_params=_cparams(("arbitrary", "arbitrary")),
        name="merge",
    )(h, mods, g, oa, ob, oc, wgl, bg, wa, wb, wc, wo)


def _prep_ffn(w1, w2):
    d, f2 = w1.shape
    f = f2 // 2
    fp = -(-f // MXU_WIDTH) * MXU_WIDTH
    nc = fp // MXU_WIDTH

    def chunk(w):
        w = jnp.pad(w.astype(BF16), ((0, 0), (0, fp - f)))
        return w.reshape(d, nc, MXU_WIDTH).transpose(1, 0, 2)

    w2p = jnp.pad(w2.astype(BF16), ((0, fp - f), (0, 0))).reshape(nc, MXU_WIDTH, d)
    return chunk(w1[:, :f]), chunk(w1[:, f:]), w2p


def _prep_win(w_in):
    d = w_in.shape[0]
    n_qkv = 3 * A_WIDTH + 3 * B_WIDTH
    w = w_in.astype(BF16)
    wf = jnp.pad(w[:, n_qkv:n_qkv + B_HEADS], ((0, 0), (0, LANES - B_HEADS)))
    uv0 = n_qkv + B_HEADS
    w_small = jnp.concatenate([w[:, :n_qkv], w[:, uv0:uv0 + 2 * C_WIDTH], wf], axis=1)
    return w_small, w[:, uv0 + 2 * C_WIDTH:]


def _pair_layouts(f_row, blk):
    b, hh, t = f_row.shape
    f4 = f_row.reshape(b, hh // 2, 2, t)
    col = f4.transpose(0, 1, 3, 2)
    rowb = f4.reshape(b, hh // 2, 2, t // blk, blk).transpose(0, 1, 3, 2, 4)
    return col, rowb


def kernel(x_prompt, x_sample, c_prompt, c_sample, cache_a_k, cache_a_v, cache_b_k, cache_b_v, cache_b_logf,
           w_ada, b_ada, norm_g, ffn_w1, ffn_w2, w_in, b_forget, b_gate, a_rel_bias, c_vnorm_g, c_spatial_w,
           c_spatial_b, w_br_a, w_br_b, w_br_c, w_out):
    depth = w_ada.shape[0]
    nb, t, d = x_prompt.shape
    ns, tn, _ = x_sample.shape
    past = cache_b_k.shape[2]
    a_rows = cache_a_k.shape[2]

    tt_p = min(t, 512)
    bb_s = max(1, min(ns, 512 // tn))
    tq = min(t, 256)
    band_rows = min(t, 4 * CHUNK)
    tk_s = min(past, 512)

    mods = _ada(jnp.concatenate([c_prompt, c_sample], axis=0), w_ada.astype(BF16), b_ada)
    mods = mods.reshape(depth, nb + ns, N_MOD, d)

    kp5 = cache_b_k.reshape(depth, ns, past, B_WIDTH)
    vp5 = cache_b_v.reshape(depth, ns, past, B_WIDTH)

    lp = min(t, C_CHUNK)
    ls = min(tn, C_CHUNK)
    tbl_p_back = A_WINDOW
    hp, hs = x_prompt, x_sample
    st_p = [[] for _ in range(5)]
    st_s = [[] for _ in range(6)]
    for l in range(depth):
        mp, ms = mods[l, :nb], mods[l, nb:]
        g = norm_g[l]
        ffn = [_prep_ffn(ffn_w1[l, i], ffn_w2[l, i]) for i in range(2)]
        w_small, w_gl = _prep_win(w_in[l])
        bf = b_forget[l].reshape(1, B_HEADS)
        cg = c_vnorm_g[l].reshape(1, C_WIDTH)
        bg = b_gate[l].reshape(1, N_BRANCH * d)
        wa, wb, wc, wo = (w.astype(BF16) for w in (w_br_a[l], w_br_b[l], w_br_c[l], w_out[l]))

        def spatial(length):
            ws = c_spatial_w[l][:, :length, :length]
            sb = jnp.repeat(c_spatial_b[l][:, :length].T, C_GROUP_W, axis=1)
            return ws, sb

        hp = _ffn(hp, mp, g, *ffn[0], 0, 1, tt_p)
        hs = _ffn(hs, ms, g, *ffn[0], 0, bb_s, tn)

        qa, ka, va, qb, kb, vb, logf, oc = _inproj(hp, mp, g, w_small, bf, cg, *spatial(lp), lp, False, 1, tt_p)
        f_row = _cumsum(logf.transpose(0, 2, 1))
        f_col, f_rowb = _pair_layouts(f_row, tq)
        ob = _fox_prompt(qb, kb, vb, f_col, f_rowb, tq)
        tbl = _band_table(a_rel_bias[l], band_rows, band_rows + A_WINDOW, tbl_p_back)
        oa = _band(qa, ka, va, tbl, A_WINDOW, band_rows)
        hp = _merge(hp, mp, g, oa, ob, oc, w_gl, bg, wa, wb, wc, wo, 1, tt_p)
        rows = min(A_WINDOW, t)
        for lst, val in zip(st_p, (ka[:, t - rows:].reshape(nb, rows, A_HEADS, HEAD_DIM),
                                   va[:, t - rows:].reshape(nb, rows, A_HEADS, HEAD_DIM),
                                   kb.reshape(nb, t, B_HEADS, HEAD_DIM), vb.reshape(nb, t, B_HEADS, HEAD_DIM), logf)):
            lst.append(val)

        qa, ka, va, qb, kb, vb, logf, oc, vc = _inproj(hs, ms, g, w_small, bf, cg, *spatial(ls), ls, True, bb_s, tn)
        lf_all = jnp.concatenate([cache_b_logf[l], logf], axis=1).transpose(0, 2, 1)
        t_all = past + tn
        t_pad = -(-t_all // MXU_WIDTH) * MXU_WIDTH
        f_all = _cumsum(jnp.pad(lf_all, ((0, 0), (0, 0), (0, t_pad - t_all))))
        f_col, _ = _pair_layouts(f_all[:, :, past:t_all], tn)
        _, f_row_past = _pair_layouts(f_all[:, :, :past], tk_s)
        f_row_new = f_all[:, :, past:t_all].reshape(ns, B_HEADS // 2, 2, tn)
        ob = _fox_sample(qb, kp5, vp5, l, kb, vb, f_col, f_row_past, f_row_new, tk_s)
        ka_all = jnp.concatenate([cache_a_k[l].reshape(ns, a_rows, A_WIDTH), ka], axis=1)
        va_all = jnp.concatenate([cache_a_v[l].reshape(ns, a_rows, A_WIDTH), va], axis=1)
        tbl = _band_table(a_rel_bias[l], tn, a_rows + tn, a_rows)
        oa = _band(qa, ka_all, va_all, tbl, 0, tn)
        hs = _merge(hs, ms, g, oa, ob, oc, w_gl, bg, wa, wb, wc, wo, bb_s, tn)
        for lst, val in zip(st_s, (ka_all[:, tn:].reshape(ns, a_rows, A_HEADS, HEAD_DIM),
                                   va_all[:, tn:].reshape(ns, a_rows, A_HEADS, HEAD_DIM),
                                   kb.reshape(ns, tn, B_HEADS, HEAD_DIM), vb.reshape(ns, tn, B_HEADS, HEAD_DIM),
                                   logf, vc)):
            lst.append(val)

        hp = _ffn(hp, mp, g, *ffn[1], 2, 1, tt_p)
        hs = _ffn(hs, ms, g, *ffn[1], 2, bb_s, tn)

    return (hp, hs, *[jnp.stack(s, 0) for s in st_p], *[jnp.stack(s, 0) for s in st_s])
```

```python
import functools

import numpy as np
import jax
import jax.numpy as jnp
from jax import lax
from jax.experimental import pallas as pl
from jax.experimental.pallas import tpu as pltpu

F32 = jnp.float32
BF16 = jnp.bfloat16

CHUNK = 64
HEAD_DIM = 64
A_HEADS = 4
A_WIDTH = A_HEADS * HEAD_DIM
A_BAND_CHUNKS = 8
A_WINDOW = A_BAND_CHUNKS * CHUNK
REL_MAX = 128
B_HEADS = 8
B_WIDTH = B_HEADS * HEAD_DIM
C_GROUPS = 4
C_GROUP_W = 64
C_WIDTH = C_GROUPS * C_GROUP_W
C_CHUNK = 128
N_BRANCH = 3
N_MOD = 9
FFN_RES = 0.5
RMS_EPS = 1e-6
NEG_INF = -1e30
QK_SCALE = HEAD_DIM ** -0.5
LOG2E = 1.4426950408889634

LANES = 128
SUBLANES = 8
MXU_WIDTH = 256
PAIR = 2 * HEAD_DIM
VMEM_LIMIT = 56 * 1024 * 1024

OFF_A = 0
OFF_B = 3 * A_WIDTH
OFF_UV = OFF_B + 3 * B_WIDTH
OFF_F = OFF_UV + 2 * C_WIDTH
W_SMALL = OFF_F + LANES


def _cparams(sem):
    return pltpu.CompilerParams(dimension_semantics=sem, vmem_limit_bytes=VMEM_LIMIT)


def _const_spec(shape):
    nd = len(shape)
    return pl.BlockSpec(shape, lambda *_: (0,) * nd, pipeline_mode=pl.Buffered(1))


def _rms(x, g):
    return x * lax.rsqrt(jnp.mean(x * x, axis=-1, keepdims=True) + RMS_EPS) * g


def _dot(a, b):
    return jnp.dot(a, b, preferred_element_type=F32)


def _dot_nt(a, b):
    return lax.dot_general(a, b, (((1,), (1,)), ((), ())), preferred_element_type=F32)


def _pre_mod(h_ref, mod_ref, g_ref, k):
    x = h_ref[...]
    bb, tt, d = x.shape
    shift = mod_ref[:, 3 * k:3 * k + 1, :]
    scale = mod_ref[:, 3 * k + 1:3 * k + 2, :]
    xm = _rms(x, g_ref[2 * k:2 * k + 1, :]) * (1.0 + scale) + shift
    return x, xm.reshape(bb * tt, d).astype(BF16)


def _post_add(x, y, mod_ref, g_ref, k, weight):
    bb, tt, d = x.shape
    gate = mod_ref[:, 3 * k + 2:3 * k + 3, :]
    yn = _rms(y, g_ref[2 * k + 1:2 * k + 2, :]).reshape(bb, tt, d)
    return x + (weight * gate) * yn


def _ada_kernel(c_ref, w_ref, b_ref, o_ref):
    c = c_ref[...]
    a = (c * jax.nn.sigmoid(c)).astype(BF16)
    o_ref[0] = _dot(a, w_ref[0]) + b_ref[0]


def _ada(c_all, w_ada, b_ada):
    depth, d, n = w_ada.shape
    nb = c_all.shape[0]
    tn = n // 4
    return pl.pallas_call(
        _ada_kernel,
        grid=(depth, n // tn),
        in_specs=[
            pl.BlockSpec((nb, d), lambda l, j: (0, 0)),
            pl.BlockSpec((1, d, tn), lambda l, j: (l, 0, j)),
            pl.BlockSpec((1, 1, tn), lambda l, j: (l, 0, j)),
        ],
        out_specs=pl.BlockSpec((1, nb, tn), lambda l, j: (l, 0, j)),
        out_shape=jax.ShapeDtypeStruct((depth, nb, n), F32),
        compiler_params=_cparams(("arbitrary", "arbitrary")),
        name="ada",
    )(c_all, w_ada, b_ada.reshape(depth, 1, n))


def _ffn_kernel(k, h_ref, mod_ref, g_ref, w1g_ref, w1u_ref, w2_ref, o_ref, xm_scr, acc_scr):
    x, xm = _pre_mod(h_ref, mod_ref, g_ref, k)
    xm_scr[...] = xm
    acc_scr[...] = jnp.zeros_like(acc_scr)

    def body(c, carry):
        xb = xm_scr[...]
        g = _dot(xb, w1g_ref[c])
        u = _dot(xb, w1u_ref[c])
        a = (g * jax.nn.sigmoid(g) * u).astype(BF16)
        acc_scr[...] += _dot(a, w2_ref[c])
        return carry

    lax.fori_loop(0, w2_ref.shape[0], body, 0, unroll=True)
    o_ref[...] = _post_add(x, acc_scr[...], mod_ref, g_ref, k, FFN_RES)


def _ffn(h, mods, g, w1g, w1u, w2, k, bb, tt):
    b, t, d = h.shape
    m = bb * tt
    return pl.pallas_call(
        functools.partial(_ffn_kernel, k),
        grid=(b // bb, t // tt),
        in_specs=[
            pl.BlockSpec((bb, tt, d), lambda i, j: (i, j, 0)),
            pl.BlockSpec((bb, N_MOD, d), lambda i, j: (i, 0, 0)),
            _const_spec(g.shape),
            _const_spec(w1g.shape),
            _const_spec(w1u.shape),
            _const_spec(w2.shape),
        ],
        out_specs=pl.BlockSpec((bb, tt, d), lambda i, j: (i, j, 0)),
        out_shape=jax.ShapeDtypeStruct(h.shape, F32),
        scratch_shapes=[pltpu.VMEM((m, d), BF16), pltpu.VMEM((m, d), F32)],
        compiler_params=_cparams(("arbitrary", "arbitrary")),
        name="ffn",
    )(h, mods, g, w1g, w1u, w2)


def _spatial_gate(u, vc, ws_ref, sb_ref, chunk_len):
    m = u.shape[0]
    vcb = vc.astype(BF16)
    row = lax.broadcasted_iota(jnp.int32, (chunk_len, chunk_len), 0)
    col = lax.broadcasted_iota(jnp.int32, (chunk_len, chunk_len), 1)
    wms = [jnp.where(row >= col, ws_ref[gi], 0.0).astype(BF16) for gi in range(C_GROUPS)]
    lane_group = lax.broadcasted_iota(jnp.int32, (chunk_len, C_WIDTH), 1) // C_GROUP_W
    pieces = []
    for r in range(m // chunk_len):
        vr = vcb[r * chunk_len:(r + 1) * chunk_len, :]
        mixed = sb_ref[...]
        for gi in range(C_GROUPS):
            mixed = mixed + jnp.where(lane_group == gi, _dot(wms[gi], vr), 0.0)
        pieces.append(mixed)
    return (u * jnp.concatenate(pieces, axis=0)).astype(BF16)


def _inproj_prompt_kernel(chunk_len, n_alias, h_ref, mod_ref, g_ref, wq_ref, wkv_ref, wf_ref, wuv_ref, bf_ref,
                          cg_ref, ws_ref, sb_ref, *refs):
    qa_ref, qb_ref, ka_ref, va_ref, kb_ref, vb_ref, lf_ref, oc_ref = refs[n_alias:]
    _, xm = _pre_mod(h_ref, mod_ref, g_ref, 1)
    tt = xm.shape[0]

    qa_ref[0] = _dot(xm, wq_ref[:, :A_WIDTH]).astype(BF16)
    qb_ref[0] = _dot(xm, wq_ref[:, A_WIDTH:]).astype(BF16)

    ka_ref[0] = _dot_nt(wkv_ref[0:A_WIDTH, :], xm).reshape(A_HEADS, HEAD_DIM, tt)
    va_ref[0] = _dot_nt(wkv_ref[A_WIDTH:2 * A_WIDTH, :], xm).reshape(A_HEADS, HEAD_DIM, tt)
    o = 2 * A_WIDTH
    kb_ref[0, 0] = _dot_nt(wkv_ref[o:o + B_WIDTH, :], xm).reshape(B_HEADS, HEAD_DIM, tt)
    vb_ref[0, 0] = _dot_nt(wkv_ref[o + B_WIDTH:o + 2 * B_WIDTH, :], xm).reshape(B_HEADS, HEAD_DIM, tt)

    lf_ref[0, 0] = jax.nn.log_sigmoid(_dot_nt(wf_ref[...], xm) + bf_ref[...])

    puv = _dot(xm, wuv_ref[...])
    vc = _rms(puv[:, C_WIDTH:], cg_ref[...])
    oc_ref[0] = _spatial_gate(puv[:, :C_WIDTH], vc, ws_ref, sb_ref, chunk_len)


def _inproj_prompt(h, mods, g, wq, wkv, wf, wuv, bf, cg, ws, sb, chunk_len, tt, layer, depth, stacked):
    b, t, d = h.shape
    tok = lambda n: pl.BlockSpec((1, tt, n), lambda i, j: (i, j, 0))
    in_specs = [
        pl.BlockSpec((1, tt, d), lambda i, j: (i, j, 0)),
        pl.BlockSpec((1, N_MOD, d), lambda i, j: (i, 0, 0)),
        _const_spec(g.shape), _const_spec(wq.shape), _const_spec(wkv.shape), _const_spec(wf.shape),
        _const_spec(wuv.shape), _const_spec(bf.shape), _const_spec(cg.shape), _const_spec(ws.shape),
        _const_spec(sb.shape),
    ]
    args = [h, mods, g, wq, wkv, wf, wuv, bf, cg, ws, sb]
    aliases = {}
    n_alias = 0
    if stacked is not None:
        n_alias = len(stacked)
        for n, buf in enumerate(stacked):
            aliases[len(args)] = 4 + n
            in_specs.append(pl.BlockSpec(memory_space=pl.ANY))
            args.append(buf)
    out_specs = [
        tok(A_WIDTH), tok(B_WIDTH),
        pl.BlockSpec((1, A_HEADS, HEAD_DIM, tt), lambda i, j: (i, 0, 0, j)),
        pl.BlockSpec((1, A_HEADS, HEAD_DIM, tt), lambda i, j: (i, 0, 0, j)),
        pl.BlockSpec((1, 1, B_HEADS, HEAD_DIM, tt), lambda i, j: (layer, i, 0, 0, j)),
        pl.BlockSpec((1, 1, B_HEADS, HEAD_DIM, tt), lambda i, j: (layer, i, 0, 0, j)),
        pl.BlockSpec((1, 1, B_HEADS, tt), lambda i, j: (layer, i, 0, j)),
        tok(C_WIDTH),
    ]
    out_shape = [
        jax.ShapeDtypeStruct((b, t, A_WIDTH), BF16), jax.ShapeDtypeStruct((b, t, B_WIDTH), BF16),
        jax.ShapeDtypeStruct((b, A_HEADS, HEAD_DIM, t), F32), jax.ShapeDtypeStruct((b, A_HEADS, HEAD_DIM, t), F32),
        jax.ShapeDtypeStruct((depth, b, B_HEADS, HEAD_DIM, t), F32),
        jax.ShapeDtypeStruct((depth, b, B_HEADS, HEAD_DIM, t), F32),
        jax.ShapeDtypeStruct((depth, b, B_HEADS, t), F32),
        jax.ShapeDtypeStruct((b, t, C_WIDTH), BF16),
    ]
    return pl.pallas_call(
        functools.partial(_inproj_prompt_kernel, chunk_len, n_alias),
        grid=(b, t // tt),
        in_specs=in_specs,
        out_specs=out_specs,
        out_shape=out_shape,
        input_output_aliases=aliases,
        compiler_params=_cparams(("arbitrary", "arbitrary")),
        name="inproj_prompt",
    )(*args)


def _inproj_sample_kernel(chunk_len, h_ref, mod_ref, g_ref, w_ref, bf_ref, cg_ref, ws_ref, sb_ref,
                          qa_ref, ka_ref, va_ref, qb_ref, kb_ref, vb_ref, lf_ref, oc_ref, vc_ref):
    _, xm = _pre_mod(h_ref, mod_ref, g_ref, 1)
    bb, tt, _ = h_ref.shape

    pa = _dot(xm, w_ref[:, OFF_A:OFF_B])
    qa_ref[...] = pa[:, :A_WIDTH].astype(BF16).reshape(bb, tt, A_WIDTH)
    ka_ref[...] = pa[:, A_WIDTH:2 * A_WIDTH].reshape(bb, tt, A_WIDTH)
    va_ref[...] = pa[:, 2 * A_WIDTH:].reshape(bb, tt, A_WIDTH)

    pb = _dot(xm, w_ref[:, OFF_B:OFF_UV])
    qb_ref[...] = pb[:, :B_WIDTH].astype(BF16).reshape(bb, tt, B_WIDTH)
    kb_ref[...] = pb[:, B_WIDTH:2 * B_WIDTH].reshape(bb, tt, B_WIDTH)
    vb_ref[...] = pb[:, 2 * B_WIDTH:].reshape(bb, tt, B_WIDTH)

    fb = _dot(xm, w_ref[:, OFF_F:W_SMALL])[:, :B_HEADS] + bf_ref[...]
    lf_ref[...] = jax.nn.log_sigmoid(fb).reshape(bb, tt, B_HEADS)

    puv = _dot(xm, w_ref[:, OFF_UV:OFF_F])
    vc = _rms(puv[:, C_WIDTH:], cg_ref[...])
    vc_ref[...] = vc.reshape(bb, tt, C_WIDTH)
    oc_ref[...] = _spatial_gate(puv[:, :C_WIDTH], vc, ws_ref, sb_ref, chunk_len).reshape(bb, tt, C_WIDTH)


def _inproj_sample(h, mods, g, w_small, b_forget, c_vnorm_g, ws, sb, chunk_len, bb, tt):
    b, t, d = h.shape

    def tok(n, dt):
        return (pl.BlockSpec((bb, tt, n), lambda i, j: (i, j, 0)), jax.ShapeDtypeStruct((b, t, n), dt))

    outs = [tok(A_WIDTH, BF16), tok(A_WIDTH, F32), tok(A_WIDTH, F32),
            tok(B_WIDTH, BF16), tok(B_WIDTH, F32), tok(B_WIDTH, F32),
            tok(B_HEADS, F32), tok(C_WIDTH, BF16), tok(C_WIDTH, F32)]
    return pl.pallas_call(
        functools.partial(_inproj_sample_kernel, chunk_len),
        grid=(b // bb, t // tt),
        in_specs=[
            pl.BlockSpec((bb, tt, d), lambda i, j: (i, j, 0)),
            pl.BlockSpec((bb, N_MOD, d), lambda i, j: (i, 0, 0)),
            _const_spec(g.shape),
            _const_spec(w_small.shape),
            _const_spec(b_forget.shape),
            _const_spec(c_vnorm_g.shape),
            _const_spec(ws.shape),
            _const_spec(sb.shape),
        ],
        out_specs=[o[0] for o in outs],
        out_shape=[o[1] for o in outs],
        compiler_params=_cparams(("arbitrary", "arbitrary")),
        name="inproj_sample",
    )(h, mods, g, w_small, b_forget, c_vnorm_g, ws, sb)


def _cumsum_kernel(blk, x_ref, o_ref):
    t = x_ref.shape[-1]
    row = lax.broadcasted_iota(jnp.int32, (blk, blk), 0)
    col = lax.broadcasted_iota(jnp.int32, (blk, blk), 1)
    tri = (row <= col).astype(F32)
    carry = jnp.zeros((x_ref.shape[-2], 1), F32)
    for j in range(t // blk):
        y = jnp.dot(x_ref[0, 0, :, j * blk:(j + 1) * blk], tri, preferred_element_type=F32,
                    precision=lax.Precision.HIGHEST) + carry
        o_ref[0, :, j * blk:(j + 1) * blk] = y
        carry = y[:, blk - 1:blk]


def _cumsum(x, layer):
    _, b, hh, t = x.shape
    blk = MXU_WIDTH
    return pl.pallas_call(
        functools.partial(_cumsum_kernel, blk),
        grid=(b,),
        in_specs=[pl.BlockSpec((1, 1, hh, t), lambda i: (layer, i, 0, 0))],
        out_specs=pl.BlockSpec((1, hh, t), lambda i: (i, 0, 0)),
        out_shape=jax.ShapeDtypeStruct((b, hh, t), F32),
        compiler_params=_cparams(("arbitrary",)),
        name="cumsum",
    )(x)


N_BIAS_ROWS = 3


def _lane(shape):
    return lax.broadcasted_iota(jnp.int32, shape, len(shape) - 1)


def _head_select(q, hh):
    return jnp.where((_lane(q.shape) >= HEAD_DIM) == bool(hh), q, jnp.zeros_like(q))


def _head_augment(q, hh, n_ones):
    lane = _lane(q.shape)
    other = HEAD_DIM * (1 - hh)
    ones = (lane >= other) & (lane < other + n_ones)
    return jnp.where((lane >= HEAD_DIM) == bool(hh), q, jnp.where(ones, 1.0, 0.0).astype(q.dtype))


def _merge_heads(o0, o1):
    return jnp.where(_lane(o0.shape) < HEAD_DIM, o0, o1)


def _split3(x):
    hi = x.astype(BF16).astype(F32)
    mid = (x - hi).astype(BF16).astype(F32)
    lo = (x - hi - mid).astype(BF16).astype(F32)
    return hi, mid, lo


def _stack_head(kt, extra_rows, hh):
    n = kt.shape[1]
    fill = jnp.zeros((HEAD_DIM - extra_rows.shape[0], n), F32)
    other = jnp.concatenate([extra_rows, fill], axis=0)
    own = kt[hh * HEAD_DIM:(hh + 1) * HEAD_DIM]
    return jnp.concatenate([own, other] if hh == 0 else [other, own], axis=0).astype(BF16)


def _values_with_ones(vt, hh):
    n = vt.shape[1]
    ones = jnp.ones((HEAD_DIM, n), F32)
    own = vt[hh * HEAD_DIM:(hh + 1) * HEAD_DIM]
    return jnp.concatenate([own, ones] if hh == 0 else [ones, own], axis=0).astype(BF16)


def _normalize(acc):
    return acc / pltpu.roll(acc, HEAD_DIM, 1)


def _online_step2(m, acc, s, vt):
    m_new = jnp.maximum(m, jnp.max(s, axis=-1, keepdims=True))
    alpha = jnp.exp2(m - m_new)
    p = jnp.exp2(s - m_new).astype(BF16)
    return m_new, alpha * acc + _dot_nt(p, vt)


def _causal(s, offset=0):
    row = lax.broadcasted_iota(jnp.int32, s.shape, 0)
    col = lax.broadcasted_iota(jnp.int32, s.shape, 1)
    return jnp.where(col <= row + offset, s, NEG_INF)


def _fox_prompt_kernel(tk, q_ref, k_ref, v_ref, f_ref, o_ref, ka_scr, va_scr):
    qi = pl.program_id(2)
    tq = q_ref.shape[1]
    t = k_ref.shape[-1]
    nk = t // tk

    @pl.when(qi == 0)
    def _():
        kt = k_ref[0, 0].reshape(PAIR, t) * LOG2E
        vt = v_ref[0, 0].reshape(PAIR, t)
        nf = f_ref[0, 0] * (-LOG2E)
        for hh in range(2):
            extra = jnp.concatenate(_split3(nf[hh:hh + 1]), axis=0)
            extra = jnp.concatenate([extra, jnp.zeros((SUBLANES - N_BIAS_ROWS, t), F32)], axis=0)
            ka = _stack_head(kt, extra, hh)
            va = _values_with_ones(vt, hh)
            for c in range(nk):
                ka_scr[hh, c] = ka[:, c * tk:(c + 1) * tk]
                va_scr[hh, c] = va[:, c * tk:(c + 1) * tk]

    q = q_ref[0] * QK_SCALE
    qh = [_head_augment(q, hh, N_BIAS_ROWS) for hh in range(2)]
    ratio = tq // tk

    def body(j, carry):
        return tuple(_online_step2(m, acc, _dot(qh[hh], ka_scr[hh, j]), va_scr[hh, j])
                     for hh, (m, acc) in enumerate(carry))

    init = tuple((jnp.full((tq, 1), NEG_INF, F32), jnp.zeros((tq, PAIR), F32)) for _ in range(2))
    carry = lax.fori_loop(0, qi * ratio, body, init)
    for d in range(ratio):
        j = qi * ratio + d
        carry = tuple(_online_step2(m, acc, _causal(_dot(qh[hh], ka_scr[hh, j]), -d * tk), va_scr[hh, j])
                      for hh, (m, acc) in enumerate(carry))
    o_ref[0] = _merge_heads(_normalize(carry[0][1]), _normalize(carry[1][1])).astype(BF16)


def _fox_prompt(q, kt_all, vt_all, f4, layer, tq, tk):
    b, t, w = q.shape
    npair = w // PAIR
    return pl.pallas_call(
        functools.partial(_fox_prompt_kernel, tk),
        grid=(b, npair, t // tq),
        in_specs=[
            pl.BlockSpec((1, tq, PAIR), lambda i, p, j: (i, j, p)),
            pl.BlockSpec((1, 1, 2, HEAD_DIM, t), lambda i, p, j: (layer, i, p, 0, 0)),
            pl.BlockSpec((1, 1, 2, HEAD_DIM, t), lambda i, p, j: (layer, i, p, 0, 0)),
            pl.BlockSpec((1, 1, 2, t), lambda i, p, j: (i, p, 0, 0)),
        ],
        out_specs=pl.BlockSpec((1, tq, PAIR), lambda i, p, j: (i, j, p)),
        out_shape=jax.ShapeDtypeStruct(q.shape, BF16),
        scratch_shapes=[pltpu.VMEM((2, t // tk, PAIR, tk), BF16), pltpu.VMEM((2, t // tk, PAIR, tk), BF16)],
        compiler_params=_cparams(("arbitrary", "arbitrary", "arbitrary")),
        name="fox_prompt",
    )(q, kt_all, vt_all, f4)


def _fox_sample_kernel(tk, q_ref, kp_ref, vp_ref, kn_ref, vn_ref, fp_ref, fn_ref, o_ref, kb_scr, vb_scr):
    tq = q_ref.shape[1]
    past = kp_ref.shape[-1]
    kb_scr[...] = (kp_ref[0, 0].reshape(PAIR, past) * LOG2E).astype(BF16)
    vb_scr[...] = vp_ref[0, 0].reshape(PAIR, past).astype(BF16)
    q = q_ref[0] * QK_SCALE
    kn = (kn_ref[0] * LOG2E).astype(BF16)
    vn = vn_ref[0].astype(BF16)
    nfp = fp_ref[0, 0] * (-LOG2E)
    nfn = fn_ref[0, 0] * (-LOG2E)
    outs = []
    for hh in range(2):
        qh = _head_select(q, hh)
        m = jnp.full((tq, 1), NEG_INF, F32)
        l = jnp.zeros((tq, 1), F32)
        acc = jnp.zeros((tq, PAIR), F32)

        def update(m, l, acc, s, pv):
            m_new = jnp.maximum(m, jnp.max(s, axis=-1, keepdims=True))
            alpha = jnp.exp2(m - m_new)
            p = jnp.exp2(s - m_new)
            return m_new, alpha * l + jnp.sum(p, axis=-1, keepdims=True), alpha * acc + pv(p.astype(BF16))

        for c in range(past // tk):
            sl = slice(c * tk, (c + 1) * tk)
            s = _dot(qh, kb_scr[:, sl]) + nfp[hh:hh + 1, sl]
            m, l, acc = update(m, l, acc, s, lambda p, sl=sl: _dot_nt(p, vb_scr[:, sl]))
        s = _causal(_dot_nt(qh, kn) + nfn[hh:hh + 1, :])
        m, l, acc = update(m, l, acc, s, lambda p: _dot(p, vn))
        outs.append(acc / l)
    o_ref[0] = _merge_heads(outs[0], outs[1]).astype(BF16)


def _fox_sample(q, kt_past, vt_past, layer, k_new, v_new, f_past, f_new, tk):
    b, tq, w = q.shape
    npair = w // PAIR
    past = kt_past.shape[-1]
    return pl.pallas_call(
        functools.partial(_fox_sample_kernel, tk),
        grid=(b, npair),
        in_specs=[
            pl.BlockSpec((1, tq, PAIR), lambda i, p: (i, 0, p)),
            pl.BlockSpec((1, 1, 2, HEAD_DIM, past), lambda i, p: (layer, i, p, 0, 0)),
            pl.BlockSpec((1, 1, 2, HEAD_DIM, past), lambda i, p: (layer, i, p, 0, 0)),
            pl.BlockSpec((1, tq, PAIR), lambda i, p: (i, 0, p)),
            pl.BlockSpec((1, tq, PAIR), lambda i, p: (i, 0, p)),
            pl.BlockSpec((1, 1, 2, past), lambda i, p: (i, p, 0, 0)),
            pl.BlockSpec((1, 1, 2, tq), lambda i, p: (i, p, 0, 0)),
        ],
        out_specs=pl.BlockSpec((1, tq, PAIR), lambda i, p: (i, 0, p)),
        out_shape=jax.ShapeDtypeStruct(q.shape, BF16),
        scratch_shapes=[pltpu.VMEM((PAIR, past), BF16), pltpu.VMEM((PAIR, past), BF16)],
        compiler_params=_cparams(("arbitrary", "arbitrary")),
        name="fox_sample",
    )(q, kt_past, vt_past, k_new, v_new, f_past, f_new)


def _band_table_kernel(rows, window, back, base_ref, o_ref):
    wp = base_ref.shape[-1]
    base = jnp.broadcast_to(base_ref[0], (rows, wp))
    shifted = pltpu.roll(base, 0, 1, stride=1, stride_axis=0)
    i = lax.broadcasted_iota(jnp.int32, (rows, window), 0)
    j = lax.broadcasted_iota(jnp.int32, (rows, window), 1)
    qc = i // CHUNK
    kc = j // CHUNK - back // CHUNK
    mask = (kc <= qc) & (kc >= qc - A_BAND_CHUNKS)
    o_ref[0] = jnp.where(mask, shifted[:, :window] * LOG2E, NEG_INF)


def _band_table(rel_bias, rows, window, back):
    heads = rel_bias.shape[0]
    wp = -(-(window + rows - 1) // LANES) * LANES
    x = np.arange(wp)
    u = np.where(x < window, x, x - wp)
    idx = np.clip(back - u, -(CHUNK - 1), REL_MAX) + (CHUNK - 1)
    base = rel_bias[:, idx].reshape(heads, 1, wp)
    return pl.pallas_call(
        functools.partial(_band_table_kernel, rows, window, back),
        grid=(heads,),
        in_specs=[pl.BlockSpec((1, 1, wp), lambda h: (h, 0, 0))],
        out_specs=pl.BlockSpec((1, rows, window), lambda h: (h, 0, 0)),
        out_shape=jax.ShapeDtypeStruct((heads, rows, window), F32),
        compiler_params=_cparams(("arbitrary",)),
        name="band_table",
    )(base)


def _band_prompt_kernel(pad_blocks, n_win, q_ref, k_ref, v_ref, tbl_ref, o_ref, ka_scr, va_scr):
    qi = pl.program_id(2)
    rows = q_ref.shape[1]
    t = k_ref.shape[-1]

    @pl.when(qi == 0)
    def _():
        kt = k_ref[0].reshape(PAIR, t) * LOG2E
        vt = v_ref[0].reshape(PAIR, t)
        gone = jnp.concatenate([jnp.full((1, rows), NEG_INF, F32), jnp.zeros((SUBLANES - 1, rows), F32)], axis=0)
        none = jnp.zeros((SUBLANES, t), F32)
        for hh in range(2):
            ka = _stack_head(kt, none, hh)
            va = _values_with_ones(vt, hh)
            before = _stack_head(jnp.zeros((PAIR, rows), F32), gone, hh)
            for c in range(pad_blocks):
                ka_scr[hh, c] = before
                va_scr[hh, c] = jnp.zeros((PAIR, rows), BF16)
            for c in range(t // rows):
                ka_scr[hh, pad_blocks + c] = ka[:, c * rows:(c + 1) * rows]
                va_scr[hh, pad_blocks + c] = va[:, c * rows:(c + 1) * rows]

    q = q_ref[0] * QK_SCALE
    outs = []
    for hh in range(2):
        qh = _head_augment(q, hh, 1)
        s = [_dot(qh, ka_scr[hh, qi + c]) + tbl_ref[hh, :, c * rows:(c + 1) * rows] for c in range(n_win)]
        m = functools.reduce(jnp.maximum, [jnp.max(x, axis=-1, keepdims=True) for x in s])
        acc = None
        for c in range(n_win):
            pv = _dot_nt(jnp.exp2(s[c] - m).astype(BF16), va_scr[hh, qi + c])
            acc = pv if acc is None else acc + pv
        outs.append(_normalize(acc))
    o_ref[0] = _merge_heads(outs[0], outs[1]).astype(BF16)


def _band_prompt(q, kt, vt, tbl, rows):
    b, t, w = q.shape
    npair = w // PAIR
    n_win = tbl.shape[2] // rows
    pad_blocks = n_win - 1
    nblk = pad_blocks + t // rows
    return pl.pallas_call(
        functools.partial(_band_prompt_kernel, pad_blocks, n_win),
        grid=(b, npair, t // rows),
        in_specs=[
            pl.BlockSpec((1, rows, PAIR), lambda i, p, j: (i, j, p)),
            pl.BlockSpec((1, 2, HEAD_DIM, t), lambda i, p, j: (i, p, 0, 0)),
            pl.BlockSpec((1, 2, HEAD_DIM, t), lambda i, p, j: (i, p, 0, 0)),
            pl.BlockSpec((2, rows, tbl.shape[2]), lambda i, p, j: (p, 0, 0)),
        ],
        out_specs=pl.BlockSpec((1, rows, PAIR), lambda i, p, j: (i, j, p)),
        out_shape=jax.ShapeDtypeStruct(q.shape, BF16),
        scratch_shapes=[pltpu.VMEM((2, nblk, PAIR, rows), BF16), pltpu.VMEM((2, nblk, PAIR, rows), BF16)],
        compiler_params=_cparams(("arbitrary", "arbitrary", "arbitrary")),
        name="band_prompt",
    )(q, kt, vt, tbl)


def _band_sample_kernel(q_ref, kp_ref, vp_ref, kn_ref, vn_ref, tbl_ref, o_ref):
    past = kp_ref.shape[-1]
    q = q_ref[0] * QK_SCALE
    kp = (kp_ref[0, 0].reshape(PAIR, past) * LOG2E).astype(BF16)
    vp = vp_ref[0, 0].reshape(PAIR, past).astype(BF16)
    kn = (kn_ref[0] * LOG2E).astype(BF16)
    vn = vn_ref[0].astype(BF16)
    outs = []
    for hh in range(2):
        qh = _head_select(q, hh)
        sp = _dot(qh, kp) + tbl_ref[hh, :, :past]
        sn = _dot_nt(qh, kn) + tbl_ref[hh, :, past:]
        m = jnp.maximum(jnp.max(sp, axis=-1, keepdims=True), jnp.max(sn, axis=-1, keepdims=True))
        pp = jnp.exp2(sp - m)
        pn = jnp.exp2(sn - m)
        l = jnp.sum(pp, axis=-1, keepdims=True) + jnp.sum(pn, axis=-1, keepdims=True)
        outs.append((_dot_nt(pp.astype(BF16), vp) + _dot(pn.astype(BF16), vn)) / l)
    o_ref[0] = _merge_heads(outs[0], outs[1]).astype(BF16)


def _band_sample(q, kt_past, vt_past, layer, k_new, v_new, tbl):
    b, tq, w = q.shape
    npair = w // PAIR
    past = kt_past.shape[-1]
    return pl.pallas_call(
        _band_sample_kernel,
        grid=(b, npair),
        in_specs=[
            pl.BlockSpec((1, tq, PAIR), lambda i, p: (i, 0, p)),
            pl.BlockSpec((1, 1, 2, HEAD_DIM, past), lambda i, p: (layer, i, p, 0, 0)),
            pl.BlockSpec((1, 1, 2, HEAD_DIM, past), lambda i, p: (layer, i, p, 0, 0)),
            pl.BlockSpec((1, tq, PAIR), lambda i, p: (i, 0, p)),
            pl.BlockSpec((1, tq, PAIR), lambda i, p: (i, 0, p)),
            pl.BlockSpec((2, tq, past + tq), lambda i, p: (p, 0, 0)),
        ],
        out_specs=pl.BlockSpec((1, tq, PAIR), lambda i, p: (i, 0, p)),
        out_shape=jax.ShapeDtypeStruct(q.shape, BF16),
        compiler_params=_cparams(("arbitrary", "arbitrary")),
        name="band_sample",
    )(q, kt_past, vt_past, k_new, v_new, tbl)


def _merge_kernel(h_ref, mod_ref, g_ref, oa_ref, ob_ref, oc_ref, wgl_ref, bg_ref, wa_ref, wb_ref, wc_ref,
                  wo_ref, o_ref):
    x, xm = _pre_mod(h_ref, mod_ref, g_ref, 1)
    bb, tt, d = x.shape
    m = bb * tt
    mix = None
    for idx, (br_ref, w_ref) in enumerate(((oa_ref, wa_ref), (ob_ref, wb_ref), (oc_ref, wc_ref))):
        gate = jax.nn.sigmoid(_dot(xm, wgl_ref[:, idx * d:(idx + 1) * d]) + bg_ref[:, idx * d:(idx + 1) * d])
        br = br_ref[...]
        term = gate * _dot(br.reshape(m, br.shape[-1]), w_ref[...])
        mix = term if mix is None else mix + term
    y = _dot(mix.astype(BF16), wo_ref[...])
    o_ref[...] = _post_add(x, y, mod_ref, g_ref, 1, 1.0)


def _merge(h, mods, g, oa, ob, oc, wgl, bg, wa, wb, wc, wo, bb, tt):
    b, t, d = h.shape

    def tok(n):
        return pl.BlockSpec((bb, tt, n), lambda i, j: (i, j, 0))

    return pl.pallas_call(
        _merge_kernel,
        grid=(b // bb, t // tt),
        in_specs=[
            tok(d),
            pl.BlockSpec((bb, N_MOD, d), lambda i, j: (i, 0, 0)),
            _const_spec(g.shape),
            tok(A_WIDTH), tok(B_WIDTH), tok(C_WIDTH),
            _const_spec(wgl.shape), _const_spec(bg.shape), _const_spec(wa.shape),
            _const_spec(wb.shape), _const_spec(wc.shape), _const_spec(wo.shape),
        ],
        out_specs=tok(d),
        out_shape=jax.ShapeDtypeStruct(h.shape, F32),
        compiler_params=_cparams(("arbitrary", "arbitrary")),
        name="merge",
    )(h, mods, g, oa, ob, oc, wgl, bg, wa, wb, wc, wo)


def _prep_ffn(w1, w2):
    d, f2 = w1.shape
    f = f2 // 2
    fp = -(-f // MXU_WIDTH) * MXU_WIDTH
    nc = fp // MXU_WIDTH

    def chunk(w):
        w = jnp.pad(w.astype(BF16), ((0, 0), (0, fp - f)))
        return w.reshape(d, nc, MXU_WIDTH).transpose(1, 0, 2)

    w2p = jnp.pad(w2.astype(BF16), ((0, fp - f), (0, 0))).reshape(nc, MXU_WIDTH, d)
    return chunk(w1[:, :f]), chunk(w1[:, f:]), w2p


def _prep_win(w_in):
    n_qkv = 3 * A_WIDTH + 3 * B_WIDTH
    w = w_in.astype(BF16)
    qa, ka, va = (w[:, i * A_WIDTH:(i + 1) * A_WIDTH] for i in range(3))
    qb, kb, vb = (w[:, OFF_B + i * B_WIDTH:OFF_B + (i + 1) * B_WIDTH] for i in range(3))
    wf = w[:, n_qkv:n_qkv + B_HEADS]
    uv0 = n_qkv + B_HEADS
    wuv = w[:, uv0:uv0 + 2 * C_WIDTH]
    w_small = jnp.concatenate([w[:, :n_qkv], wuv, jnp.pad(wf, ((0, 0), (0, LANES - B_HEADS)))], axis=1)
    wq = jnp.concatenate([qa, qb], axis=1)
    wkv_t = jnp.concatenate([ka, va, kb, vb], axis=1).T
    return w_small, wq, wkv_t, wf.T, wuv, w[:, uv0 + 2 * C_WIDTH:]


def kernel(x_prompt, x_sample, c_prompt, c_sample, cache_a_k, cache_a_v, cache_b_k, cache_b_v, cache_b_logf,
           w_ada, b_ada, norm_g, ffn_w1, ffn_w2, w_in, b_forget, b_gate, a_rel_bias, c_vnorm_g, c_spatial_w,
           c_spatial_b, w_br_a, w_br_b, w_br_c, w_out):
    depth = w_ada.shape[0]
    nb, t, d = x_prompt.shape
    ns, tn, _ = x_sample.shape
    past = cache_b_k.shape[2]
    a_rows = cache_a_k.shape[2]

    tt_p = min(t, 512)
    bb_s = max(1, min(ns, 512 // tn))
    tq = min(t, 512)
    tk = min(t, 512)
    band_rows = min(t, 4 * CHUNK)
    tk_s = min(past, 512)
    rows_out = min(A_WINDOW, t)

    mods = _ada(jnp.concatenate([c_prompt, c_sample], axis=0), w_ada.astype(BF16), b_ada)
    mods = mods.reshape(depth, nb + ns, N_MOD, d)

    cak_t = cache_a_k.transpose(0, 1, 3, 4, 2)
    cav_t = cache_a_v.transpose(0, 1, 3, 4, 2)
    cbk_t = cache_b_k.transpose(0, 1, 3, 4, 2)
    cbv_t = cache_b_v.transpose(0, 1, 3, 4, 2)
    clf_t = cache_b_logf.transpose(0, 1, 3, 2)

    lp = min(t, C_CHUNK)
    ls = min(tn, C_CHUNK)
    t_all = past + tn
    t_pad = -(-t_all // MXU_WIDTH) * MXU_WIDTH

    hp, hs = x_prompt, x_sample
    stacked = None
    ak_p, av_p = [], []
    st_s = [[] for _ in range(6)]
    for l in range(depth):
        mp, ms = mods[l, :nb], mods[l, nb:]
        g = norm_g[l]
        ffn = [_prep_ffn(ffn_w1[l, i], ffn_w2[l, i]) for i in range(2)]
        w_small, wq, wkv_t, wf_t, wuv, w_gl = _prep_win(w_in[l])
        bf_row = b_forget[l].reshape(1, B_HEADS)
        bf_col = b_forget[l].reshape(B_HEADS, 1)
        cg = c_vnorm_g[l].reshape(1, C_WIDTH)
        bg = b_gate[l].reshape(1, N_BRANCH * d)
        wa, wb, wc, wo = (w.astype(BF16) for w in (w_br_a[l], w_br_b[l], w_br_c[l], w_out[l]))

        def spatial(length):
            ws = c_spatial_w[l][:, :length, :length]
            sb = jnp.repeat(c_spatial_b[l][:, :length].T, C_GROUP_W, axis=1)
            return ws, sb

        hp = _ffn(hp, mp, g, *ffn[0], 0, 1, tt_p)
        hs = _ffn(hs, ms, g, *ffn[0], 0, bb_s, tn)

        qa, qb, ka_t, va_t, kb_all, vb_all, lf_all, oc = _inproj_prompt(
            hp, mp, g, wq, wkv_t, wf_t, wuv, bf_col, cg, *spatial(lp), lp, tt_p, l, depth, stacked)
        stacked = (kb_all, vb_all, lf_all)
        f_row = _cumsum(lf_all, l)
        ob = _fox_prompt(qb, kb_all, vb_all, f_row.reshape(nb, B_HEADS // 2, 2, t), l, tq, tk)
        tbl = _band_table(a_rel_bias[l], band_rows, band_rows + A_WINDOW, A_WINDOW)
        oa = _band_prompt(qa, ka_t, va_t, tbl, band_rows)
        hp = _merge(hp, mp, g, oa, ob, oc, w_gl, bg, wa, wb, wc, wo, 1, tt_p)
        ak_p.append(ka_t[..., t - rows_out:])
        av_p.append(va_t[..., t - rows_out:])

        qa, ka, va, qb, kb, vb, logf, oc, vc = _inproj_sample(hs, ms, g, w_small, bf_row, cg, *spatial(ls), ls,
                                                              bb_s, tn)
        lf_row = jnp.concatenate([clf_t[l], logf.transpose(0, 2, 1)], axis=2)
        f_all = _cumsum(jnp.pad(lf_row, ((0, 0), (0, 0), (0, t_pad - t_all)))[None], 0)
        f_past = f_all[:, :, :past].reshape(ns, B_HEADS // 2, 2, past)
        f_new = f_all[:, :, past:t_all].reshape(ns, B_HEADS // 2, 2, tn)
        ob = _fox_sample(qb, cbk_t, cbv_t, l, kb, vb, f_past, f_new, tk_s)
        tbl = _band_table(a_rel_bias[l], tn, a_rows + tn, a_rows)
        oa = _band_sample(qa, cak_t, cav_t, l, ka, va, tbl)
        hs = _merge(hs, ms, g, oa, ob, oc, w_gl, bg, wa, wb, wc, wo, bb_s, tn)

        def rolled(cache_t, new):
            new_t = new.reshape(ns, tn, A_HEADS, HEAD_DIM).transpose(0, 2, 3, 1)
            return jnp.concatenate([cache_t[..., tn:], new_t], axis=-1)

        for lst, val in zip(st_s, (rolled(cak_t[l], ka), rolled(cav_t[l], va),
                                   kb.reshape(ns, tn, B_HEADS, HEAD_DIM), vb.reshape(ns, tn, B_HEADS, HEAD_DIM),
                                   logf, vc)):
            lst.append(val)

        hp = _ffn(hp, mp, g, *ffn[1], 2, 1, tt_p)
        hs = _ffn(hs, ms, g, *ffn[1], 2, bb_s, tn)

    kb_all, vb_all, lf_all = stacked
    to_time_major = lambda x: x.transpose(0, 1, 4, 2, 3)
    sample_out = [jnp.stack(s, 0) for s in st_s]
    return (hp, hs,
            to_time_major(jnp.stack(ak_p, 0)), to_time_major(jnp.stack(av_p, 0)),
            to_time_major(kb_all), to_time_major(vb_all), lf_all.transpose(0, 1, 3, 2),
            to_time_major(sample_out[0]), to_time_major(sample_out[1]), *sample_out[2:])
```
